```python
import jax, jax.numpy as jnp
from jax import lax
import numpy as np

D_MODEL = 1024
BATCH = 8
SEQ = 2048
DEPTH = 1
DEC_BATCH = 128
DEC_SEQ = 1
PAST_LEN = 16384
PAGE_SIZE = 128

D_MIX = D_MODEL
D_A = D_MIX // 2
D_B = D_MIX - D_A
N_HEADS_A = 8
HEAD_DIM_A = D_A // N_HEADS_A
N_GROUPS_B = 8
CHUNK = 128
CONV_W = 3
D_FF = 4 * D_MODEL
N_MOD = 6
D_IN = 2 * D_A + 3 * D_B
EPS = 1e-6

kernel_name = 'hymba_gmlp_shortconv_step'


def rmsnorm(x, g):
    xf = x.astype(jnp.float32)
    y = xf * lax.rsqrt(jnp.mean(jnp.square(xf), axis=-1, keepdims=True) + EPS)
    return (y * g.astype(jnp.float32)).astype(x.dtype)


def chunk_spatial_mix(v, w_s, b_s):
    bsz, t, h, dh = v.shape
    n_chunks = -(-t // CHUNK)
    pad = n_chunks * CHUNK - t
    vp = jnp.pad(v, ((0, 0), (0, pad), (0, 0), (0, 0)))
    vc = vp.reshape(bsz, n_chunks, CHUNK, h, dh)
    wm = jnp.tril(w_s)
    out = jnp.einsum('hts,bnshd->bnthd', wm, vc) + jnp.transpose(b_s)[None, None, :, :, None]
    return out.reshape(bsz, n_chunks * CHUNK, h, dh)[:, :t]


def short_conv(z, prev, w_conv):
    t = z.shape[1]
    zp = jnp.concatenate([prev, z], axis=1)
    y = w_conv[0] * zp[:, 0:t]
    for k in range(1, CONV_W):
        y = y + w_conv[k] * zp[:, k:k + t]
    return y, zp[:, -(CONV_W - 1):]


def layer(x, c, conv_prev, g_mix, w_ada, b_ada, w_in, g_v, w_s, b_s, w_conv, w_out,
          g_ffn, w_ff1, w_ff2):
    bsz, t, _ = x.shape
    mod = jax.nn.silu(c) @ w_ada + b_ada
    sh1, sc1, gt1, sh2, sc2, gt2 = [m[:, None, :] for m in jnp.split(mod, N_MOD, axis=-1)]

    h = rmsnorm(x, g_mix) * (1.0 + sc1) + sh1
    p = h @ w_in
    u, v, bg, cg, hin = jnp.split(p, [D_A, 2 * D_A, 2 * D_A + D_B, 2 * D_A + 2 * D_B], axis=-1)
    u = jax.nn.gelu(u)
    v = rmsnorm(jax.nn.gelu(v).reshape(bsz, t, N_HEADS_A, HEAD_DIM_A),
                g_v.reshape(N_HEADS_A, HEAD_DIM_A))
    a_out = u * chunk_spatial_mix(v, w_s, b_s).reshape(bsz, t, D_A)
    y_conv, new_conv = short_conv(cg * hin, conv_prev, w_conv)
    b_out = bg * y_conv
    mix = jnp.concatenate([a_out, b_out], axis=-1) @ w_out
    x = x + gt1 * mix

    h2 = rmsnorm(x, g_ffn) * (1.0 + sc2) + sh2
    f = jnp.square(jax.nn.relu(h2 @ w_ff1)) @ w_ff2
    x = x + gt2 * f
    return x, v.reshape(bsz, t, D_A), new_conv


def setup_inputs(seed: int = 0) -> dict:
    key = jax.random.key(seed)
    ks = jax.random.split(key, 20)
    nrm = lambda k, s: jax.random.normal(k, s, jnp.float32)
    return {
        'x_prompt': nrm(ks[0], (BATCH, SEQ, D_MODEL)),
        'x_sample': nrm(ks[1], (DEC_BATCH, DEC_SEQ, D_MODEL)),
        'c_prompt': nrm(ks[2], (BATCH, D_MODEL)),
        'c_sample': nrm(ks[3], (DEC_BATCH, D_MODEL)),
        'state_conv': nrm(ks[4], (DEPTH, DEC_BATCH, CONV_W - 1, D_B)),
        'g_mix': 1.0 + 0.02 * nrm(ks[5], (DEPTH, D_MODEL)),
        'w_ada': nrm(ks[6], (DEPTH, D_MODEL, N_MOD * D_MODEL)) * D_MODEL ** -0.5,
        'b_ada': 0.02 * nrm(ks[7], (DEPTH, N_MOD * D_MODEL)),
        'w_in': nrm(ks[8], (DEPTH, D_MODEL, D_IN)) * D_MODEL ** -0.5,
        'g_v': 1.0 + 0.02 * nrm(ks[9], (DEPTH, D_A)),
        'w_s': nrm(ks[10], (DEPTH, N_HEADS_A, CHUNK, CHUNK)) * CHUNK ** -0.5,
        'b_s': 1.0 + 0.02 * nrm(ks[11], (DEPTH, N_HEADS_A, CHUNK)),
        'w_conv': nrm(ks[12], (DEPTH, CONV_W, D_B)) * CONV_W ** -0.5,
        'w_out': nrm(ks[13], (DEPTH, D_MIX, D_MODEL)) * D_MIX ** -0.5,
        'g_ffn': 1.0 + 0.02 * nrm(ks[14], (DEPTH, D_MODEL)),
        'w_ff1': nrm(ks[15], (DEPTH, D_MODEL, D_FF)) * D_MODEL ** -0.5,
        'w_ff2': nrm(ks[16], (DEPTH, D_FF, D_MODEL)) * D_FF ** -0.5,
        'g_final': 1.0 + 0.02 * nrm(ks[17], (D_MODEL,)),
    }


def reference(x_prompt, x_sample, c_prompt, c_sample, state_conv, g_mix, w_ada, b_ada,
              w_in, g_v, w_s, b_s, w_conv, w_out, g_ffn, w_ff1, w_ff2, g_final):
    xp, xs = x_prompt, x_sample
    conv_p_list, conv_s_list, v_s_list = [], [], []
    for l in range(DEPTH):
        wl = (g_mix[l], w_ada[l], b_ada[l], w_in[l], g_v[l], w_s[l], b_s[l], w_conv[l],
              w_out[l], g_ffn[l], w_ff1[l], w_ff2[l])
        prev_p = jnp.zeros((xp.shape[0], CONV_W - 1, D_B), xp.dtype)
        xp, _, conv_p = layer(xp, c_prompt, prev_p, *wl)
        xs, v_s, conv_s = layer(xs, c_sample, state_conv[l], *wl)
        conv_p_list.append(conv_p)
        conv_s_list.append(conv_s)
        v_s_list.append(v_s)
    y_prompt = rmsnorm(xp, g_final)
    y_sample = rmsnorm(xs, g_final)
    new_conv_prompt = jnp.stack(conv_p_list)
    new_conv_sample = jnp.stack(conv_s_list)
    new_chunk_v_sample = jnp.stack(v_s_list)
    return (y_prompt, y_sample, new_conv_prompt, new_conv_sample, new_chunk_v_sample)
```

```python
import functools

import jax
import jax.numpy as jnp
from jax import lax
from jax.experimental import pallas as pl
from jax.experimental.pallas import tpu as pltpu

D_MODEL = 1024
D_A = 512
D_B = 512
N_HEADS_A = 8
HEAD_DIM_A = 64
CHUNK = 128
CONV_W = 3
D_FF = 4096
N_MOD = 6
D_IN = 2 * D_A + 3 * D_B
EPS = 1e-6

LANES = 128
SUBLANES = 8
TOKEN_TILE = 512
FF_CHUNK = 512
MOD_TILE = 1536
VMEM_LIMIT_BYTES = 56 * 1024 * 1024

_BF16 = jnp.bfloat16
_F32 = jnp.float32


def _dot(a, b):
    return jnp.dot(a.astype(_BF16), b.astype(_BF16), preferred_element_type=_F32)


def _rms(x, g):
    return x * lax.rsqrt(jnp.mean(x * x, axis=-1, keepdims=True) + EPS) * g


def _head_norm_gelu(pv, hsum, gv):
    vg = jax.nn.gelu(pv)
    ms = _dot(vg * vg, hsum)
    return vg * lax.rsqrt(ms + EPS) * gv


def _ffn(h2, ff1_ref, ff2_ref):
    h2b = h2.astype(_BF16)
    acc = None
    for j in range(D_FF // FF_CHUNK):
        lo = j * FF_CHUNK
        hid = jnp.dot(h2b, ff1_ref[:, lo:lo + FF_CHUNK], preferred_element_type=_F32)
        hid = jnp.square(jnp.maximum(hid, 0.0)).astype(_BF16)
        part = jnp.dot(hid, ff2_ref[lo:lo + FF_CHUNK, :], preferred_element_type=_F32)
        acc = part if acc is None else acc + part
    return acc


def _mod_kernel(c_ref, w_ref, b_ref, o_ref):
    c = c_ref[...]
    o_ref[...] = _dot(c * jax.nn.sigmoid(c), w_ref[...]) + b_ref[...]


def _prompt_kernel(x_ref, mod_ref, gmix_ref, win_ref, gv_ref, ws_ref, bias_ref, wconv_ref,
                   wout_ref, gffn_ref, ff1_ref, ff2_ref, gfin_ref, hsum_ref,
                   y_ref, conv_ref, wm_scr, zbuf):
    tm = x_ref.shape[1]
    b = pl.program_id(0)
    t = pl.program_id(1)

    @pl.when((b == 0) & (t == 0))
    def _():
        row = lax.broadcasted_iota(jnp.int32, (CHUNK, CHUNK), 0)
        col = lax.broadcasted_iota(jnp.int32, (CHUNK, CHUNK), 1)
        for h in range(N_HEADS_A):
            wm = jnp.where(col <= row, ws_ref[h], 0.0).astype(_BF16)
            wm_scr[h // 2, :, (h % 2) * CHUNK:(h % 2 + 1) * CHUNK] = wm

    @pl.when(t == 0)
    def _():
        zbuf[0:SUBLANES, :] = jnp.zeros((SUBLANES, D_B), _F32)

    @pl.when(t > 0)
    def _():
        zbuf[0:SUBLANES, :] = zbuf[tm:tm + SUBLANES, :]

    x = x_ref[0]
    mod = mod_ref[0]
    sh1, sc1, gt1 = mod[0:1], mod[1:2], mod[2:3]
    sh2, sc2, gt2 = mod[3:4], mod[4:5], mod[5:6]

    h = _rms(x, gmix_ref[...]) * (1.0 + sc1) + sh1
    p = jnp.dot(h.astype(_BF16), win_ref[...], preferred_element_type=_F32)

    u = jax.nn.gelu(p[:, 0:D_A])
    v = _head_norm_gelu(p[:, D_A:2 * D_A], hsum_ref[...], gv_ref[...])

    lane = lax.broadcasted_iota(jnp.int32, (CHUNK, LANES), 1)
    low_head = lane < HEAD_DIM_A
    rows = []
    for c in range(tm // CHUNK):
        cols = []
        for j in range(N_HEADS_A // 2):
            vp = v[c * CHUNK:(c + 1) * CHUNK, j * LANES:(j + 1) * LANES]
            rhs = jnp.concatenate([jnp.where(low_head, vp, 0.0), jnp.where(low_head, 0.0, vp)], axis=0)
            cols.append(jnp.dot(wm_scr[j], rhs.astype(_BF16), preferred_element_type=_F32))
        rows.append(jnp.concatenate(cols, axis=1) + bias_ref[...])
    a_out = u * jnp.concatenate(rows, axis=0)

    z = p[:, 2 * D_A + D_B:2 * D_A + 2 * D_B] * p[:, 2 * D_A + 2 * D_B:D_IN]
    zbuf[SUBLANES:SUBLANES + tm, :] = z
    wc = wconv_ref[...]
    y_conv = (wc[0:1] * zbuf[SUBLANES - 2:SUBLANES - 2 + tm, :]
              + wc[1:2] * zbuf[SUBLANES - 1:SUBLANES - 1 + tm, :]
              + wc[2:3] * z)
    b_out = p[:, 2 * D_A:2 * D_A + D_B] * y_conv

    mixed = jnp.concatenate([a_out, b_out], axis=1)
    x1 = x + gt1 * jnp.dot(mixed.astype(_BF16), wout_ref[...], preferred_element_type=_F32)

    h2 = _rms(x1, gffn_ref[...]) * (1.0 + sc2) + sh2
    x2 = x1 + gt2 * _ffn(h2, ff1_ref, ff2_ref)
    y_ref[0] = _rms(x2, gfin_ref[...])

    @pl.when(t == pl.num_programs(1) - 1)
    def _():
        conv_ref[0] = zbuf[SUBLANES + tm - 2:SUBLANES + tm, :]


def _sample_kernel(x_ref, mod_ref, prev_ref, gmix_ref, win_ref, gv_ref, s0_ref, b0_ref, wconv_ref,
                   wout_ref, gffn_ref, ff1_ref, ff2_ref, gfin_ref, hsum_ref,
                   y_ref, conv_ref, v_ref):
    x = x_ref[...]
    mod = mod_ref[...]
    sh1, sc1, gt1, sh2, sc2, gt2 = [mod[:, i * D_MODEL:(i + 1) * D_MODEL] for i in range(N_MOD)]

    h = _rms(x, gmix_ref[...]) * (1.0 + sc1) + sh1
    p = jnp.dot(h.astype(_BF16), win_ref[...], preferred_element_type=_F32)

    u = jax.nn.gelu(p[:, 0:D_A])
    v = _head_norm_gelu(p[:, D_A:2 * D_A], hsum_ref[...], gv_ref[...])
    v_ref[...] = v
    a_out = u * (s0_ref[...] * v + b0_ref[...])

    z = p[:, 2 * D_A + D_B:2 * D_A + 2 * D_B] * p[:, 2 * D_A + 2 * D_B:D_IN]
    prev = prev_ref[...]
    wc = wconv_ref[...]
    y_conv = wc[0:1] * prev[:, 0:D_B] + wc[1:2] * prev[:, D_B:2 * D_B] + wc[2:3] * z
    b_out = p[:, 2 * D_A:2 * D_A + D_B] * y_conv
    conv_ref[:, 0:D_B] = prev[:, D_B:2 * D_B]
    conv_ref[:, D_B:2 * D_B] = z

    mixed = jnp.concatenate([a_out, b_out], axis=1)
    x1 = x + gt1 * jnp.dot(mixed.astype(_BF16), wout_ref[...], preferred_element_type=_F32)

    h2 = _rms(x1, gffn_ref[...]) * (1.0 + sc2) + sh2
    x2 = x1 + gt2 * _ffn(h2, ff1_ref, ff2_ref)
    y_ref[...] = _rms(x2, gfin_ref[...])


def _resident(shape):
    return pl.BlockSpec(shape, lambda *_: (0,) * len(shape), pipeline_mode=pl.Buffered(1))


def kernel(x_prompt, x_sample, c_prompt, c_sample, state_conv, g_mix, w_ada, b_ada, w_in, g_v,
           w_s, b_s, w_conv, w_out, g_ffn, w_ff1, w_ff2, g_final):
    batch, seq, d = x_prompt.shape
    dec = x_sample.shape[0]
    assert d == D_MODEL and x_sample.shape[1] == 1 and seq % TOKEN_TILE == 0
    assert g_mix.shape[0] == 1, "single trunk layer"

    c_all = jnp.concatenate([c_prompt, c_sample], axis=0)
    n_c = c_all.shape[0]
    n_mod = N_MOD * D_MODEL
    mod = pl.pallas_call(
        _mod_kernel,
        grid=(n_mod // MOD_TILE,),
        in_specs=[pl.BlockSpec((n_c, D_MODEL), lambda j: (0, 0)),
                  pl.BlockSpec((D_MODEL, MOD_TILE), lambda j: (0, j)),
                  pl.BlockSpec((1, MOD_TILE), lambda j: (0, j))],
        out_specs=pl.BlockSpec((n_c, MOD_TILE), lambda j: (0, j)),
        out_shape=jax.ShapeDtypeStruct((n_c, n_mod), _F32),
        compiler_params=pltpu.CompilerParams(dimension_semantics=("arbitrary",),
                                             vmem_limit_bytes=VMEM_LIMIT_BYTES),
        name="adaln_mod",
    )(c_all, w_ada[0], b_ada)
    mod_p = mod[:batch].reshape(batch, N_MOD, D_MODEL)
    mod_s = mod[batch:]

    win = w_in[0].astype(_BF16)
    wout = w_out[0].astype(_BF16)
    ff1 = w_ff1[0].astype(_BF16)
    ff2 = w_ff2[0].astype(_BF16)
    head_of_lane = jnp.arange(D_A) // HEAD_DIM_A
    hsum = ((head_of_lane[:, None] == head_of_lane[None, :]).astype(_F32) / HEAD_DIM_A).astype(_BF16)
    bias = jnp.repeat(b_s[0].T, HEAD_DIM_A, axis=1)
    s0 = jnp.repeat(w_s[0, :, 0, 0], HEAD_DIM_A)[None, :]
    b0 = bias[0:1]
    gv = g_v

    weights_specs = dict(
        gmix=_resident((1, D_MODEL)), win=_resident((D_MODEL, D_IN)), gv=_resident((1, D_A)),
        wconv=_resident((CONV_W, D_B)), wout=_resident((D_MODEL, D_MODEL)),
        gffn=_resident((1, D_MODEL)), ff1=_resident((D_MODEL, D_FF)), ff2=_resident((D_FF, D_MODEL)),
        gfin=_resident((1, D_MODEL)), hsum=_resident((D_A, D_A)))
    ws = weights_specs

    y_prompt, conv_p = pl.pallas_call(
        _prompt_kernel,
        grid=(batch, seq // TOKEN_TILE),
        in_specs=[pl.BlockSpec((1, TOKEN_TILE, D_MODEL), lambda b, t: (b, t, 0)),
                  pl.BlockSpec((1, N_MOD, D_MODEL), lambda b, t: (b, 0, 0)),
                  ws["gmix"], ws["win"], ws["gv"],
                  _resident((N_HEADS_A, CHUNK, CHUNK)), _resident((CHUNK, D_A)),
                  ws["wconv"], ws["wout"], ws["gffn"], ws["ff1"], ws["ff2"], ws["gfin"], ws["hsum"]],
        out_specs=[pl.BlockSpec((1, TOKEN_TILE, D_MODEL), lambda b, t: (b, t, 0)),
                   pl.BlockSpec((1, CONV_W - 1, D_B), lambda b, t: (b, 0, 0))],
        out_shape=[jax.ShapeDtypeStruct((batch, seq, D_MODEL), _F32),
                   jax.ShapeDtypeStruct((batch, CONV_W - 1, D_B), _F32)],
        scratch_shapes=[pltpu.VMEM((N_HEADS_A // 2, CHUNK, 2 * CHUNK), _BF16),
                        pltpu.VMEM((TOKEN_TILE + SUBLANES, D_B), _F32)],
        compiler_params=pltpu.CompilerParams(dimension_semantics=("arbitrary", "arbitrary"),
                                             vmem_limit_bytes=VMEM_LIMIT_BYTES),
        name="prompt_layer",
    )(x_prompt, mod_p, g_mix, win, gv, w_s[0], bias, w_conv[0], wout, g_ffn, ff1, ff2,
      g_final[None, :], hsum)

    whole = lambda shape: pl.BlockSpec(shape, lambda: (0,) * len(shape))
    y_s, conv_s, v_s = pl.pallas_call(
        _sample_kernel,
        in_specs=[whole((dec, D_MODEL)), whole((dec, n_mod)), whole((dec, 2 * D_B)),
                  whole((1, D_MODEL)), whole((D_MODEL, D_IN)), whole((1, D_A)),
                  whole((1, D_A)), whole((1, D_A)), whole((CONV_W, D_B)),
                  whole((D_MODEL, D_MODEL)), whole((1, D_MODEL)), whole((D_MODEL, D_FF)),
                  whole((D_FF, D_MODEL)), whole((1, D_MODEL)), whole((D_A, D_A))],
        out_specs=[whole((dec, D_MODEL)), whole((dec, 2 * D_B)), whole((dec, D_A))],
        out_shape=[jax.ShapeDtypeStruct((dec, D_MODEL), _F32),
                   jax.ShapeDtypeStruct((dec, 2 * D_B), _F32),
                   jax.ShapeDtypeStruct((dec, D_A), _F32)],
        compiler_params=pltpu.CompilerParams(vmem_limit_bytes=VMEM_LIMIT_BYTES),
        name="sample_layer",
    )(x_sample.reshape(dec, D_MODEL), mod_s, state_conv[0].reshape(dec, 2 * D_B), g_mix, win, gv,
      s0, b0, w_conv[0], wout, g_ffn, ff1, ff2, g_final[None, :], hsum)

    return (y_prompt,
            y_s.reshape(dec, 1, D_MODEL),
            conv_p[None],
            conv_s.reshape(1, dec, CONV_W - 1, D_B),
            v_s.reshape(1, dec, 1, D_A))
```

```python
import functools

import jax
import jax.numpy as jnp
from jax import lax
from jax.experimental import pallas as pl
from jax.experimental.pallas import tpu as pltpu

D_MODEL = 1024
D_A = 512
D_B = 512
N_HEADS_A = 8
HEAD_DIM_A = 64
CHUNK = 128
CONV_W = 3
D_FF = 4096
N_MOD = 6
D_IN = 2 * D_A + 3 * D_B
EPS = 1e-6

LANES = 128
SUBLANES = 8
TOKEN_TILE = 512
FF_CHUNK = 512
STAGE_ROWS = 1024
STAGE_COLS = 512
VMEM_LIMIT_BYTES = 58 * 1024 * 1024

_BF16 = jnp.bfloat16
_F32 = jnp.float32


def _rms(x, g):
    return x * lax.rsqrt(jnp.mean(x * x, axis=-1, keepdims=True) + EPS) * g


def _head_norm_gelu(pv, hsum, gv):
    vg = jax.nn.gelu(pv)
    ms = jnp.dot((vg * vg).astype(_BF16), hsum, preferred_element_type=_F32)
    return vg * lax.rsqrt(ms + EPS) * gv


def _ffn(h2, ff1_ref, ff2_ref):
    h2b = h2.astype(_BF16)
    acc = None
    for j in range(D_FF // FF_CHUNK):
        lo = j * FF_CHUNK
        hid = jnp.dot(h2b, ff1_ref[:, lo:lo + FF_CHUNK], preferred_element_type=_F32)
        hid = jnp.square(jnp.maximum(hid, 0.0)).astype(_BF16)
        part = jnp.dot(hid, ff2_ref[lo:lo + FF_CHUNK, :], preferred_element_type=_F32)
        acc = part if acc is None else acc + part
    return acc


def _stream_weights(segments, stage, sem):
    offsets = []
    total = 0
    for _, _, n, _ in segments:
        offsets.append(total)
        total += n

    def chunk_copy(seg, j, slot):
        src, n_cols = segments[seg][0], segments[seg][1]
        r, c = j // n_cols, j % n_cols
        return pltpu.make_async_copy(
            src.at[pl.ds(r * STAGE_ROWS, STAGE_ROWS), pl.ds(c * STAGE_COLS, STAGE_COLS)],
            stage.at[slot], sem.at[slot])

    def for_segment_of(k, fn):
        for seg, (off, (_, _, n, _)) in enumerate(zip(offsets, segments)):
            @pl.when((k >= off) & (k < off + n))
            def _(seg=seg, off=off):
                fn(seg, k - off)

    chunk_copy(0, 0, 0).start()

    def body(k, carry):
        slot = k % 2

        @pl.when(k + 1 < total)
        def _():
            for_segment_of(k + 1, lambda seg, j: chunk_copy(seg, j, 1 - slot).start())

        def consume(seg, j):
            chunk_copy(seg, j, slot).wait()
            n_cols = segments[seg][1]
            segments[seg][3](j // n_cols, j % n_cols, stage[slot])

        for_segment_of(k, consume)
        return carry

    lax.fori_loop(0, total, body, 0)


def _layer_kernel(xp_ref, xs_ref, prev_ref, cp_ref, cs_ref, wada_hbm, bada_ref, gmix_ref, win_hbm,
                  gv_ref, ws_ref, bias_ref, s0_ref, wconv_ref, wout_hbm, gffn_ref, ff1_hbm, ff2_hbm,
                  gfin_ref,
                  yp_ref, convp_ref, ys_ref, convs_ref, vs_ref,
                  win_s, wout_s, ff1_s, ff2_s, wm_s, hsum_s, mod_s, zbuf,
                  *, n_tiles, tiles_per_seq, batch):
    tm = xp_ref.shape[1]
    s = pl.program_id(0)

    @pl.when(s == 0)
    def _():
        row = lax.broadcasted_iota(jnp.int32, (CHUNK, CHUNK), 0)
        col = lax.broadcasted_iota(jnp.int32, (CHUNK, CHUNK), 1)
        for h in range(N_HEADS_A):
            wm = jnp.where(col <= row, ws_ref[h], 0.0).astype(_BF16)
            wm_s[h // 2, :, (h % 2) * CHUNK:(h % 2 + 1) * CHUNK] = wm
        head_shift = HEAD_DIM_A.bit_length() - 1
        hr = lax.shift_right_logical(lax.broadcasted_iota(jnp.int32, (D_A, D_A), 0), head_shift)
        hc = lax.shift_right_logical(lax.broadcasted_iota(jnp.int32, (D_A, D_A), 1), head_shift)
        hsum_s[...] = jnp.where(hr == hc, 1.0 / HEAD_DIM_A, 0.0).astype(_BF16)

        c = jnp.concatenate([cp_ref[...], cs_ref[...]], axis=0)
        cb = (c * jax.nn.sigmoid(c)).astype(_BF16)

        def mod_chunk(r, cblk, w):
            lo = pl.multiple_of(cblk * STAGE_COLS, STAGE_COLS)
            mod_s[:, pl.ds(lo, STAGE_COLS)] = (
                jnp.dot(cb, w.astype(_BF16), preferred_element_type=_F32)
                + bada_ref[:, pl.ds(lo, STAGE_COLS)])

        def cast_into(dst):
            def consume(r, cblk, w):
                dst[pl.ds(pl.multiple_of(r * STAGE_ROWS, STAGE_ROWS), STAGE_ROWS),
                    pl.ds(pl.multiple_of(cblk * STAGE_COLS, STAGE_COLS), STAGE_COLS)] = w.astype(_BF16)
            return consume

        def chunks(ref):
            return (ref.shape[1] // STAGE_COLS, (ref.shape[0] // STAGE_ROWS) * (ref.shape[1] // STAGE_COLS))

        segments = [(wada_hbm,) + chunks(wada_hbm) + (mod_chunk,),
                    (win_hbm,) + chunks(win_hbm) + (cast_into(win_s),),
                    (wout_hbm,) + chunks(wout_hbm) + (cast_into(wout_s),),
                    (ff1_hbm,) + chunks(ff1_hbm) + (cast_into(ff1_s),),
                    (ff2_hbm,) + chunks(ff2_hbm) + (cast_into(ff2_s),)]
        pl.run_scoped(functools.partial(_stream_weights, segments),
                      pltpu.VMEM((2, STAGE_ROWS, STAGE_COLS), _F32),
                      pltpu.SemaphoreType.DMA((2,)))

    @pl.when(s < n_tiles)
    def _():
        b = s // tiles_per_seq
        t = s % tiles_per_seq

        @pl.when(t == 0)
        def _():
            zbuf[0:SUBLANES, :] = jnp.zeros((SUBLANES, D_B), _F32)

        @pl.when(t > 0)
        def _():
            zbuf[0:SUBLANES, :] = zbuf[tm:tm + SUBLANES, :]

        x = xp_ref[0]
        mod = mod_s[pl.ds(b, 1), :]
        sh1, sc1, gt1, sh2, sc2, gt2 = [mod[:, i * D_MODEL:(i + 1) * D_MODEL] for i in range(N_MOD)]

        h = _rms(x, gmix_ref[...]) * (1.0 + sc1) + sh1
        p = jnp.dot(h.astype(_BF16), win_s[...], preferred_element_type=_F32)

        u = jax.nn.gelu(p[:, 0:D_A])
        v = _head_norm_gelu(p[:, D_A:2 * D_A], hsum_s[...], gv_ref[...])

        lane = lax.broadcasted_iota(jnp.int32, (CHUNK, LANES), 1)
        low_head = lane < HEAD_DIM_A
        rows = []
        for c in range(tm // CHUNK):
            cols = []
            for j in range(N_HEADS_A // 2):
                vp = v[c * CHUNK:(c + 1) * CHUNK, j * LANES:(j + 1) * LANES]
                rhs = jnp.concatenate([jnp.where(low_head, vp, 0.0), jnp.where(low_head, 0.0, vp)], axis=0)
                cols.append(jnp.dot(wm_s[j], rhs.astype(_BF16), preferred_element_type=_F32))
            rows.append(jnp.concatenate(cols, axis=1) + bias_ref[...])
        a_out = u * jnp.concatenate(rows, axis=0)

        z = p[:, 2 * D_A + D_B:2 * D_A + 2 * D_B] * p[:, 2 * D_A + 2 * D_B:D_IN]
        zbuf[SUBLANES:SUBLANES + tm, :] = z
        wc = wconv_ref[...]
        y_conv = (wc[0:1] * zbuf[SUBLANES - 2:SUBLANES - 2 + tm, :]
                  + wc[1:2] * zbuf[SUBLANES - 1:SUBLANES - 1 + tm, :]
                  + wc[2:3] * z)
        b_out = p[:, 2 * D_A:2 * D_A + D_B] * y_conv

        mixed = jnp.concatenate([a_out, b_out], axis=1)
        x1 = x + gt1 * jnp.dot(mixed.astype(_BF16), wout_s[...], preferred_element_type=_F32)

        h2 = _rms(x1, gffn_ref[...]) * (1.0 + sc2) + sh2
        x2 = x1 + gt2 * _ffn(h2, ff1_s, ff2_s)
        yp_ref[0] = _rms(x2, gfin_ref[...])

        @pl.when(t == tiles_per_seq - 1)
        def _():
            convp_ref[0] = zbuf[SUBLANES + tm - 2:SUBLANES + tm, :]

    @pl.when(s == n_tiles)
    def _():
        x = xs_ref[...]
        sh1, sc1, gt1, sh2, sc2, gt2 = [mod_s[batch:, i * D_MODEL:(i + 1) * D_MODEL] for i in range(N_MOD)]

        h = _rms(x, gmix_ref[...]) * (1.0 + sc1) + sh1
        p = jnp.dot(h.astype(_BF16), win_s[...], preferred_element_type=_F32)

        u = jax.nn.gelu(p[:, 0:D_A])
        v = _head_norm_gelu(p[:, D_A:2 * D_A], hsum_s[...], gv_ref[...])
        vs_ref[...] = v
        a_out = u * (s0_ref[...] * v + bias_ref[0:1, :])

        z = p[:, 2 * D_A + D_B:2 * D_A + 2 * D_B] * p[:, 2 * D_A + 2 * D_B:D_IN]
        prev = prev_ref[...]
        wc = wconv_ref[...]
        y_conv = wc[0:1] * prev[:, 0:D_B] + wc[1:2] * prev[:, D_B:2 * D_B] + wc[2:3] * z
        b_out = p[:, 2 * D_A:2 * D_A + D_B] * y_conv
        convs_ref[:, 0:D_B] = prev[:, D_B:2 * D_B]
        convs_ref[:, D_B:2 * D_B] = z

        mixed = jnp.concatenate([a_out, b_out], axis=1)
        x1 = x + gt1 * jnp.dot(mixed.astype(_BF16), wout_s[...], preferred_element_type=_F32)

        h2 = _rms(x1, gffn_ref[...]) * (1.0 + sc2) + sh2
        x2 = x1 + gt2 * _ffn(h2, ff1_s, ff2_s)
        ys_ref[...] = _rms(x2, gfin_ref[...])


def _resident(shape):
    return pl.BlockSpec(shape, lambda s: (0,) * len(shape), pipeline_mode=pl.Buffered(1))


def kernel(x_prompt, x_sample, c_prompt, c_sample, state_conv, g_mix, w_ada, b_ada, w_in, g_v,
           w_s, b_s, w_conv, w_out, g_ffn, w_ff1, w_ff2, g_final):
    batch, seq, d = x_prompt.shape
    dec = x_sample.shape[0]
    assert d == D_MODEL and x_sample.shape[1] == 1 and seq % TOKEN_TILE == 0
    assert g_mix.shape[0] == 1, "single trunk layer"
    assert batch % SUBLANES == 0
    tiles_per_seq = seq // TOKEN_TILE
    n_tiles = batch * tiles_per_seq
    n_mod = N_MOD * D_MODEL

    bias = jnp.repeat(b_s[0].T, HEAD_DIM_A, axis=1)
    s0 = jnp.repeat(w_s[0, :, 0, 0], HEAD_DIM_A)[None, :]

    def tile_map(s):
        tile = jnp.minimum(s, n_tiles - 1)
        return (tile // tiles_per_seq, tile % tiles_per_seq, 0)

    hbm = pl.BlockSpec(memory_space=pl.ANY)
    in_specs = [
        pl.BlockSpec((1, TOKEN_TILE, D_MODEL), tile_map),
        _resident((dec, D_MODEL)),
        _resident((dec, 2 * D_B)),
        _resident((batch, D_MODEL)), _resident((dec, D_MODEL)),
        hbm, _resident((1, n_mod)),
        _resident((1, D_MODEL)), hbm, _resident((1, D_A)),
        _resident((N_HEADS_A, CHUNK, CHUNK)), _resident((CHUNK, D_A)), _resident((1, D_A)),
        _resident((CONV_W, D_B)), hbm, _resident((1, D_MODEL)),
        hbm, hbm, _resident((1, D_MODEL)),
    ]
    out_specs = [
        pl.BlockSpec((1, TOKEN_TILE, D_MODEL), tile_map),
        pl.BlockSpec((1, CONV_W - 1, D_B), lambda s: (jnp.minimum(s, n_tiles - 1) // tiles_per_seq, 0, 0)),
        pl.BlockSpec((dec, D_MODEL), lambda s: (0, 0)),
        pl.BlockSpec((dec, 2 * D_B), lambda s: (0, 0)),
        pl.BlockSpec((dec, D_A), lambda s: (0, 0)),
    ]
    out_shape = [
        jax.ShapeDtypeStruct((batch, seq, D_MODEL), _F32),
        jax.ShapeDtypeStruct((batch, CONV_W - 1, D_B), _F32),
        jax.ShapeDtypeStruct((dec, D_MODEL), _F32),
        jax.ShapeDtypeStruct((dec, 2 * D_B), _F32),
        jax.ShapeDtypeStruct((dec, D_A), _F32),
    ]
    scratch_shapes = [
        pltpu.VMEM((D_MODEL, D_IN), _BF16), pltpu.VMEM((D_MODEL, D_MODEL), _BF16),
        pltpu.VMEM((D_MODEL, D_FF), _BF16), pltpu.VMEM((D_FF, D_MODEL), _BF16),
        pltpu.VMEM((N_HEADS_A // 2, CHUNK, 2 * CHUNK), _BF16),
        pltpu.VMEM((D_A, D_A), _BF16),
        pltpu.VMEM((batch + dec, n_mod), _F32),
        pltpu.VMEM((TOKEN_TILE + SUBLANES, D_B), _F32),
    ]
    y_prompt, conv_p, y_s, conv_s, v_s = pl.pallas_call(
        functools.partial(_layer_kernel, n_tiles=n_tiles, tiles_per_seq=tiles_per_seq, batch=batch),
        grid=(n_tiles + 1,),
        in_specs=in_specs,
        out_specs=out_specs,
        out_shape=out_shape,
        scratch_shapes=scratch_shapes,
        compiler_params=pltpu.CompilerParams(dimension_semantics=("arbitrary",),
                                             vmem_limit_bytes=VMEM_LIMIT_BYTES),
        name="decoder_layer",
    )(x_prompt, x_sample.reshape(dec, D_MODEL), state_conv.reshape(dec, 2 * D_B), c_prompt, c_sample,
      w_ada.reshape(D_MODEL, n_mod), b_ada, g_mix, w_in.reshape(D_MODEL, D_IN), g_v,
      w_s.reshape(N_HEADS_A, CHUNK, CHUNK), bias, s0, w_conv.reshape(CONV_W, D_B),
      w_out.reshape(D_MODEL, D_MODEL), g_ffn, w_ff1.reshape(D_MODEL, D_FF),
      w_ff2.reshape(D_FF, D_MODEL), g_final.reshape(1, D_MODEL))

    return (y_prompt,
            y_s.reshape(dec, 1, D_MODEL),
            conv_p[None],
            conv_s.reshape(1, dec, CONV_W - 1, D_B),
            v_s.reshape(1, dec, 1, D_A))
```

```python
import functools

import jax
import jax.numpy as jnp
from jax import lax
from jax.experimental import pallas as pl
from jax.experimental.pallas import tpu as pltpu

D_MODEL = 1024
D_A = 512
D_B = 512
N_HEADS_A = 8
HEAD_DIM_A = 64
CHUNK = 128
CONV_W = 3
D_FF = 4096
N_MOD = 6
D_IN = 2 * D_A + 3 * D_B
EPS = 1e-6

LANES = 128
SUBLANES = 8
TOKEN_TILE = 512
FF_CHUNK = 512
STAGE_ROWS = 1024
STAGE_COLS = 512
STAGE_SLOTS = 4
VMEM_LIMIT_BYTES = 58 * 1024 * 1024

_BF16 = jnp.bfloat16
_F32 = jnp.float32


def _rms(x, g):
    return x * lax.rsqrt(jnp.mean(x * x, axis=-1, keepdims=True) + EPS) * g


def _head_norm_gelu(pv, hsum, gv):
    vg = jax.nn.gelu(pv)
    ms = jnp.dot((vg * vg).astype(_BF16), hsum, preferred_element_type=_F32)
    return vg * lax.rsqrt(ms + EPS) * gv


def _ffn(h2, ff1_ref, ff2_ref):
    h2b = h2.astype(_BF16)
    acc = None
    for j in range(D_FF // FF_CHUNK):
        lo = j * FF_CHUNK
        hid = jnp.dot(h2b, ff1_ref[:, lo:lo + FF_CHUNK], preferred_element_type=_F32)
        hid = jnp.square(jnp.maximum(hid, 0.0)).astype(_BF16)
        part = jnp.dot(hid, ff2_ref[lo:lo + FF_CHUNK, :], preferred_element_type=_F32)
        acc = part if acc is None else acc + part
    return acc


def _stream_weights(segments, stage, sem):
    offsets = []
    total = 0
    for _, _, n, _ in segments:
        offsets.append(total)
        total += n

    def chunk_copy(seg, j, slot):
        src, n_cols = segments[seg][0], segments[seg][1]
        r, c = j // n_cols, j % n_cols
        return pltpu.make_async_copy(
            src.at[pl.ds(r * STAGE_ROWS, STAGE_ROWS), pl.ds(c * STAGE_COLS, STAGE_COLS)],
            stage.at[slot], sem.at[slot])

    def for_segment_of(k, fn):
        for seg, (off, (_, _, n, _)) in enumerate(zip(offsets, segments)):
            @pl.when((k >= off) & (k < off + n))
            def _(seg=seg, off=off):
                fn(seg, k - off)

    def start(k, slot):
        for_segment_of(k, lambda seg, j: chunk_copy(seg, j, slot).start(priority=slot % 2))

    def consume(k, slot):
        def fn(seg, j):
            chunk_copy(seg, j, slot).wait()
            n_cols = segments[seg][1]
            segments[seg][3](j // n_cols, j % n_cols, stage[slot])
        for_segment_of(k, fn)

    ahead = STAGE_SLOTS - 1
    for k in range(ahead):
        start(jnp.int32(k), k)

    def body(i, carry):
        for u in range(STAGE_SLOTS):
            k = i * STAGE_SLOTS + u
            start(k + ahead, (u + ahead) % STAGE_SLOTS)
            consume(k, u)
        return carry

    lax.fori_loop(0, pl.cdiv(total, STAGE_SLOTS), body, 0)


def _layer_kernel(xp_ref, xs_ref, prev_ref, cp_ref, cs_ref, wada_hbm, bada_ref, gmix_ref, win_hbm,
                  gv_ref, ws_ref, bias_ref, s0_ref, wconv_ref, wout_hbm, gffn_ref, ff1_hbm, ff2_hbm,
                  gfin_ref,
                  yp_ref, convp_ref, ys_ref, convs_ref, vs_ref,
                  win_s, wout_s, ff1_s, ff2_s, wm_s, hsum_s, mod_s, zbuf,
                  *, n_tiles, tiles_per_seq, batch):
    tm = xp_ref.shape[1]
    s = pl.program_id(0)

    @pl.when(s == 0)
    def _():
        row = lax.broadcasted_iota(jnp.int32, (CHUNK, CHUNK), 0)
        col = lax.broadcasted_iota(jnp.int32, (CHUNK, CHUNK), 1)
        for h in range(N_HEADS_A):
            wm = jnp.where(col <= row, ws_ref[h], 0.0).astype(_BF16)
            wm_s[h // 2, :, (h % 2) * CHUNK:(h % 2 + 1) * CHUNK] = wm
        head_shift = HEAD_DIM_A.bit_length() - 1
        hr = lax.shift_right_logical(lax.broadcasted_iota(jnp.int32, (D_A, D_A), 0), head_shift)
        hc = lax.shift_right_logical(lax.broadcasted_iota(jnp.int32, (D_A, D_A), 1), head_shift)
        hsum_s[...] = jnp.where(hr == hc, 1.0 / HEAD_DIM_A, 0.0).astype(_BF16)

        c = jnp.concatenate([cp_ref[...], cs_ref[...]], axis=0)
        cb = (c * jax.nn.sigmoid(c)).astype(_BF16)

        def mod_chunk(r, cblk, w):
            lo = pl.multiple_of(cblk * STAGE_COLS, STAGE_COLS)
            mod_s[:, pl.ds(lo, STAGE_COLS)] = (
                jnp.dot(cb, w.astype(_BF16), preferred_element_type=_F32)
                + bada_ref[:, pl.ds(lo, STAGE_COLS)])

        def cast_into(dst):
            def consume(r, cblk, w):
                dst[pl.ds(pl.multiple_of(r * STAGE_ROWS, STAGE_ROWS), STAGE_ROWS),
                    pl.ds(pl.multiple_of(cblk * STAGE_COLS, STAGE_COLS), STAGE_COLS)] = w.astype(_BF16)
            return consume

        def chunks(ref):
            return (ref.shape[1] // STAGE_COLS, (ref.shape[0] // STAGE_ROWS) * (ref.shape[1] // STAGE_COLS))

        segments = [(wada_hbm,) + chunks(wada_hbm) + (mod_chunk,),
                    (win_hbm,) + chunks(win_hbm) + (cast_into(win_s),),
                    (wout_hbm,) + chunks(wout_hbm) + (cast_into(wout_s),),
                    (ff1_hbm,) + chunks(ff1_hbm) + (cast_into(ff1_s),),
                    (ff2_hbm,) + chunks(ff2_hbm) + (cast_into(ff2_s),)]
        pl.run_scoped(functools.partial(_stream_weights, segments),
                      pltpu.VMEM((STAGE_SLOTS, STAGE_ROWS, STAGE_COLS), _F32),
                      pltpu.SemaphoreType.DMA((STAGE_SLOTS,)))

    @pl.when(s < n_tiles)
    def _():
        b = s // tiles_per_seq
        t = s % tiles_per_seq

        @pl.when(t == 0)
        def _():
            zbuf[0:SUBLANES, :] = jnp.zeros((SUBLANES, D_B), _F32)

        @pl.when(t > 0)
        def _():
            zbuf[0:SUBLANES, :] = zbuf[tm:tm + SUBLANES, :]

        x = xp_ref[0]
        mod = mod_s[pl.ds(b, 1), :]
        sh1, sc1, gt1, sh2, sc2, gt2 = [mod[:, i * D_MODEL:(i + 1) * D_MODEL] for i in range(N_MOD)]

        h = _rms(x, gmix_ref[...]) * (1.0 + sc1) + sh1
        p = jnp.dot(h.astype(_BF16), win_s[...], preferred_element_type=_F32)

        u = jax.nn.gelu(p[:, 0:D_A])
        v = _head_norm_gelu(p[:, D_A:2 * D_A], hsum_s[...], gv_ref[...])

        lane = lax.broadcasted_iota(jnp.int32, (CHUNK, LANES), 1)
        low_head = lane < HEAD_DIM_A
        rows = []
        for c in range(tm // CHUNK):
            cols = []
            for j in range(N_HEADS_A // 2):
                vp = v[c * CHUNK:(c + 1) * CHUNK, j * LANES:(j + 1) * LANES]
                rhs = jnp.concatenate([jnp.where(low_head, vp, 0.0), jnp.where(low_head, 0.0, vp)], axis=0)
                cols.append(jnp.dot(wm_s[j], rhs.astype(_BF16), preferred_element_type=_F32))
            rows.append(jnp.concatenate(cols, axis=1) + bias_ref[...])
        a_out = u * jnp.concatenate(rows, axis=0)

        z = p[:, 2 * D_A + D_B:2 * D_A + 2 * D_B] * p[:, 2 * D_A + 2 * D_B:D_IN]
        zbuf[SUBLANES:SUBLANES + tm, :] = z
        wc = wconv_ref[...]
        y_conv = (wc[0:1] * zbuf[SUBLANES - 2:SUBLANES - 2 + tm, :]
                  + wc[1:2] * zbuf[SUBLANES - 1:SUBLANES - 1 + tm, :]
                  + wc[2:3] * z)
        b_out = p[:, 2 * D_A:2 * D_A + D_B] * y_conv

        mixed = jnp.concatenate([a_out, b_out], axis=1)
        x1 = x + gt1 * jnp.dot(mixed.astype(_BF16), wout_s[...], preferred_element_type=_F32)

        h2 = _rms(x1, gffn_ref[...]) * (1.0 + sc2) + sh2
        x2 = x1 + gt2 * _ffn(h2, ff1_s, ff2_s)
        yp_ref[0] = _rms(x2, gfin_ref[...])

        @pl.when(t == tiles_per_seq - 1)
        def _():
            convp_ref[0] = zbuf[SUBLANES + tm - 2:SUBLANES + tm, :]

    @pl.when(s == n_tiles)
    def _():
        x = xs_ref[...]
        sh1, sc1, gt1, sh2, sc2, gt2 = [mod_s[batch:, i * D_MODEL:(i + 1) * D_MODEL] for i in range(N_MOD)]

        h = _rms(x, gmix_ref[...]) * (1.0 + sc1) + sh1
        p = jnp.dot(h.astype(_BF16), win_s[...], preferred_element_type=_F32)

        u = jax.nn.gelu(p[:, 0:D_A])
        v = _head_norm_gelu(p[:, D_A:2 * D_A], hsum_s[...], gv_ref[...])
        vs_ref[...] = v
        a_out = u * (s0_ref[...] * v + bias_ref[0:1, :])

        z = p[:, 2 * D_A + D_B:2 * D_A + 2 * D_B] * p[:, 2 * D_A + 2 * D_B:D_IN]
        prev = prev_ref[...]
        wc = wconv_ref[...]
        y_conv = wc[0:1] * prev[:, 0:D_B] + wc[1:2] * prev[:, D_B:2 * D_B] + wc[2:3] * z
        b_out = p[:, 2 * D_A:2 * D_A + D_B] * y_conv
        convs_ref[:, 0:D_B] = prev[:, D_B:2 * D_B]
        convs_ref[:, D_B:2 * D_B] = z

        mixed = jnp.concatenate([a_out, b_out], axis=1)
        x1 = x + gt1 * jnp.dot(mixed.astype(_BF16), wout_s[...], preferred_element_type=_F32)

        h2 = _rms(x1, gffn_ref[...]) * (1.0 + sc2) + sh2
        x2 = x1 + gt2 * _ffn(h2, ff1_s, ff2_s)
        ys_ref[...] = _rms(x2, gfin_ref[...])


def _resident(shape):
    return pl.BlockSpec(shape, lambda s: (0,) * len(shape), pipeline_mode=pl.Buffered(1))


def kernel(x_prompt, x_sample, c_prompt, c_sample, state_conv, g_mix, w_ada, b_ada, w_in, g_v,
           w_s, b_s, w_conv, w_out, g_ffn, w_ff1, w_ff2, g_final):
    batch, seq, d = x_prompt.shape
    dec = x_sample.shape[0]
    assert d == D_MODEL and x_sample.shape[1] == 1 and seq % TOKEN_TILE == 0
    assert g_mix.shape[0] == 1, "single trunk layer"
    assert batch % SUBLANES == 0
    tiles_per_seq = seq // TOKEN_TILE
    n_tiles = batch * tiles_per_seq
    n_mod = N_MOD * D_MODEL

    bias = jnp.repeat(b_s[0].T, HEAD_DIM_A, axis=1)
    s0 = jnp.repeat(w_s[0, :, 0, 0], HEAD_DIM_A)[None, :]

    def tile_map(s):
        tile = jnp.minimum(s, n_tiles - 1)
        return (tile // tiles_per_seq, tile % tiles_per_seq, 0)

    hbm = pl.BlockSpec(memory_space=pl.ANY)
    in_specs = [
        pl.BlockSpec((1, TOKEN_TILE, D_MODEL), tile_map),
        _resident((dec, D_MODEL)),
        _resident((dec, 2 * D_B)),
        _resident((batch, D_MODEL)), _resident((dec, D_MODEL)),
        hbm, _resident((1, n_mod)),
        _resident((1, D_MODEL)), hbm, _resident((1, D_A)),
        _resident((N_HEADS_A, CHUNK, CHUNK)), _resident((CHUNK, D_A)), _resident((1, D_A)),
        _resident((CONV_W, D_B)), hbm, _resident((1, D_MODEL)),
        hbm, hbm, _resident((1, D_MODEL)),
    ]
    out_specs = [
        pl.BlockSpec((1, TOKEN_TILE, D_MODEL), tile_map),
        pl.BlockSpec((1, CONV_W - 1, D_B), lambda s: (jnp.minimum(s, n_tiles - 1) // tiles_per_seq, 0, 0)),
        pl.BlockSpec((dec, D_MODEL), lambda s: (0, 0)),
        pl.BlockSpec((dec, 2 * D_B), lambda s: (0, 0)),
        pl.BlockSpec((dec, D_A), lambda s: (0, 0)),
    ]
    out_shape = [
        jax.ShapeDtypeStruct((batch, seq, D_MODEL), _F32),
        jax.ShapeDtypeStruct((batch, CONV_W - 1, D_B), _F32),
        jax.ShapeDtypeStruct((dec, D_MODEL), _F32),
        jax.ShapeDtypeStruct((dec, 2 * D_B), _F32),
        jax.ShapeDtypeStruct((dec, D_A), _F32),
    ]
    scratch_shapes = [
        pltpu.VMEM((D_MODEL, D_IN), _BF16), pltpu.VMEM((D_MODEL, D_MODEL), _BF16),
        pltpu.VMEM((D_MODEL, D_FF), _BF16), pltpu.VMEM((D_FF, D_MODEL), _BF16),
        pltpu.VMEM((N_HEADS_A // 2, CHUNK, 2 * CHUNK), _BF16),
        pltpu.VMEM((D_A, D_A), _BF16),
        pltpu.VMEM((batch + dec, n_mod), _F32),
        pltpu.VMEM((TOKEN_TILE + SUBLANES, D_B), _F32),
    ]
    y_prompt, conv_p, y_s, conv_s, v_s = pl.pallas_call(
        functools.partial(_layer_kernel, n_tiles=n_tiles, tiles_per_seq=tiles_per_seq, batch=batch),
        grid=(n_tiles + 1,),
        in_specs=in_specs,
        out_specs=out_specs,
        out_shape=out_shape,
        scratch_shapes=scratch_shapes,
        compiler_params=pltpu.CompilerParams(dimension_semantics=("arbitrary",),
                                             vmem_limit_bytes=VMEM_LIMIT_BYTES),
        name="decoder_layer",
    )(x_prompt, x_sample.reshape(dec, D_MODEL), state_conv.reshape(dec, 2 * D_B), c_prompt, c_sample,
      w_ada.reshape(D_MODEL, n_mod), b_ada, g_mix, w_in.reshape(D_MODEL, D_IN), g_v,
      w_s.reshape(N_HEADS_A, CHUNK, CHUNK), bias, s0, w_conv.reshape(CONV_W, D_B),
      w_out.reshape(D_MODEL, D_MODEL), g_ffn, w_ff1.reshape(D_MODEL, D_FF),
      w_ff2.reshape(D_FF, D_MODEL), g_final.reshape(1, D_MODEL))

    return (y_prompt,
            y_s.reshape(dec, 1, D_MODEL),
            conv_p[None],
            conv_s.reshape(1, dec, CONV_W - 1, D_B),
            v_s.reshape(1, dec, 1, D_A))
```
